```python
import jax, jax.numpy as jnp
from jax import lax
import numpy as np

D_MODEL = 2048
BATCH = 2
SEQ = 16384
DEPTH = 2

W_A = D_MODEL // 2
W_B = D_MODEL // 2
W_C = D_MODEL // 2
A_GROUPS = 8
B_GROUPS = 8
CONV_A = 3
CONV_B = 31
POOL_WINDOWS = (2, 4, 8, 16)
N_POOL = len(POOL_WINDOWS)
C_GROUP = W_C // N_POOL
N_BRANCH = 3
RMS_EPS = 1e-6
LN_EPS = 1e-5
SPLIT_SIZES = (W_A, W_A, W_A, W_A, W_B, W_B, W_B, W_C, W_C, N_BRANCH * D_MODEL)
IN_COLS = sum(SPLIT_SIZES)
SPLIT_POINTS = tuple(int(v) for v in np.cumsum(SPLIT_SIZES)[:-1])

kernel_name = "hybrid_gated_conv_pool_trunk"


def rmsnorm(x, g):
    x32 = x.astype(jnp.float32)
    r = x32 * lax.rsqrt(jnp.mean(x32 * x32, axis=-1, keepdims=True) + RMS_EPS)
    return (r * g.astype(jnp.float32)).astype(x.dtype)


def layernorm(x, g, b):
    x32 = x.astype(jnp.float32)
    mu = jnp.mean(x32, axis=-1, keepdims=True)
    var = jnp.mean(jnp.square(x32 - mu), axis=-1, keepdims=True)
    y = (x32 - mu) * lax.rsqrt(var + LN_EPS)
    return (y * g.astype(jnp.float32) + b.astype(jnp.float32)).astype(x.dtype)


def causal_dwconv(u, w):
    k = w.shape[0]
    return lax.conv_general_dilated(
        u, w[:, None, :].astype(u.dtype), window_strides=(1,),
        padding=((k - 1, 0),), dimension_numbers=("NWC", "WIO", "NWC"),
        feature_group_count=u.shape[-1])


def short_conv_mixer(a_u, a_b, a_c, a_z, conv_w, w_br):
    y = a_b * causal_dwconv(a_c * a_u, conv_w)
    return (jax.nn.silu(a_z) * y) @ w_br


def conformer_conv_mixer(b_v, b_g, b_z, conv_w, conv_b, ln_g, ln_b, w_br):
    v = b_v * jax.nn.sigmoid(b_g)
    v = causal_dwconv(v, conv_w) + conv_b
    v = jax.nn.silu(layernorm(v, ln_g, ln_b))
    return (jax.nn.silu(b_z) * v) @ w_br


def pool_mixer(c_u, c_z, group_w, scale, w_br):
    bsz, seq, _ = c_u.shape
    cs = lax.cumsum(c_u.astype(jnp.float32), axis=1)
    p = jnp.pad(cs, ((0, 0), (1, 0), (0, 0)))
    pos = jnp.arange(seq, dtype=jnp.float32)[:, None] + 1.0
    groups = []
    for i, w in enumerate(POOL_WINDOWS):
        pg = p[..., i * C_GROUP:(i + 1) * C_GROUP]
        upper = pg[:, 1:]
        lower = jnp.pad(pg, ((0, 0), (w - 1, 0), (0, 0)))[:, :seq]
        cnt = jnp.minimum(pos, float(w))
        groups.append((upper - lower) / cnt)
    pooled = jnp.stack(groups, axis=2).astype(c_u.dtype)
    u = c_u.reshape(bsz, seq, N_POOL, C_GROUP)
    y = jnp.einsum("bsgc,gcd->bsgd", pooled - u, group_w.astype(c_u.dtype))
    y = y.reshape(bsz, seq, W_C) * scale
    return (jax.nn.silu(c_z) * y) @ w_br


def setup_inputs(seed: int = 0) -> dict:
    key = jax.random.key(seed)
    ks = jax.random.split(key, 20)
    f32 = jnp.float32
    n = lambda k, shape, s: jax.random.normal(k, shape, f32) * s
    return {
        "x": n(ks[0], (BATCH, SEQ, D_MODEL), 1.0),
        "norm_g": 1.0 + n(ks[1], (DEPTH, D_MODEL), 0.02),
        "w_in": n(ks[2], (DEPTH, D_MODEL, IN_COLS), D_MODEL ** -0.5),
        "a_conv_w": n(ks[3], (DEPTH, CONV_A, W_A), CONV_A ** -0.5),
        "b_conv_w": n(ks[4], (DEPTH, CONV_B, W_B), CONV_B ** -0.5),
        "b_conv_b": n(ks[5], (DEPTH, W_B), 0.02),
        "b_ln_g": 1.0 + n(ks[6], (DEPTH, W_B), 0.02),
        "b_ln_b": n(ks[7], (DEPTH, W_B), 0.02),
        "c_group_w": n(ks[8], (DEPTH, N_POOL, C_GROUP, C_GROUP), C_GROUP ** -0.5),
        "c_scale": 1.0 + n(ks[9], (DEPTH, W_C), 0.02),
        "w_br_a": n(ks[10], (DEPTH, W_A, D_MODEL), W_A ** -0.5),
        "w_br_b": n(ks[11], (DEPTH, W_B, D_MODEL), W_B ** -0.5),
        "w_br_c": n(ks[12], (DEPTH, W_C, D_MODEL), W_C ** -0.5),
        "w_o": n(ks[13], (DEPTH, D_MODEL, D_MODEL), D_MODEL ** -0.5),
        "final_g": 1.0 + n(ks[14], (D_MODEL,), 0.02),
    }


def reference(x, norm_g, w_in, a_conv_w, b_conv_w, b_conv_b, b_ln_g, b_ln_b,
              c_group_w, c_scale, w_br_a, w_br_b, w_br_c, w_o, final_g):
    bsz, seq, _ = x.shape
    for l in range(DEPTH):
        h = rmsnorm(x, norm_g[l])
        proj = h @ w_in[l]
        (a_u, a_b, a_c, a_z, b_v, b_g, b_z, c_u, c_z, gates) = jnp.split(
            proj, SPLIT_POINTS, axis=-1)
        y_a = short_conv_mixer(a_u, a_b, a_c, a_z, a_conv_w[l], w_br_a[l])
        y_b = conformer_conv_mixer(b_v, b_g, b_z, b_conv_w[l], b_conv_b[l],
                                   b_ln_g[l], b_ln_b[l], w_br_b[l])
        y_c = pool_mixer(c_u, c_z, c_group_w[l], c_scale[l], w_br_c[l])
        g = jax.nn.sigmoid(gates.reshape(bsz, seq, N_BRANCH, D_MODEL))
        merged = g[:, :, 0] * y_a + g[:, :, 1] * y_b + g[:, :, 2] * y_c
        x = x + merged @ w_o[l]
    return rmsnorm(x, final_g)
```

```python
import functools

import jax
import jax.numpy as jnp
from jax import lax
from jax.experimental import pallas as pl
from jax.experimental.pallas import tpu as pltpu

D_MODEL = 2048
WIDTH = 1024
CONV_A = 3
CONV_B = 31
POOL_WINDOWS = (2, 4, 8, 16)
C_GROUP = WIDTH // len(POOL_WINDOWS)
RMS_EPS = 1e-6
LN_EPS = 1e-5

OFF_A = 0
OFF_B = 4 * WIDTH
OFF_C = 7 * WIDTH
OFF_G = 9 * WIDTH

SUBLANES = 8
LANES = 128
HALO_B = 32
HALO_C = 16
HALO_A = 8

VMEM_LIMIT = 56 * 1024 * 1024


def _params(sem):
    return pltpu.CompilerParams(dimension_semantics=sem, vmem_limit_bytes=VMEM_LIMIT)


def _dot(a, b):
    return jnp.dot(a, b, preferred_element_type=jnp.float32)


def _silu(x):
    return x * jax.nn.sigmoid(x)


def _rms_kernel(x_ref, g_ref, h_ref):
    x = x_ref[...]
    r = x * lax.rsqrt(jnp.mean(x * x, axis=-1, keepdims=True) + RMS_EPS)
    h_ref[...] = (r * g_ref[...]).astype(h_ref.dtype)


def _rmsnorm_call(x2, g, tile, out_dtype):
    n = x2.shape[0]
    return pl.pallas_call(
        _rms_kernel,
        grid=(n // tile,),
        in_specs=[pl.BlockSpec((tile, D_MODEL), lambda i: (i, 0)),
                  pl.BlockSpec((1, D_MODEL), lambda i: (0, 0))],
        out_specs=pl.BlockSpec((tile, D_MODEL), lambda i: (i, 0)),
        out_shape=jax.ShapeDtypeStruct((n, D_MODEL), out_dtype),
        compiler_params=_params(("arbitrary",)),
        name="rmsnorm",
    )(x2, g)


def _mixer_a_kernel(h_ref, wu_ref, wb_ref, wc_ref, wz_ref, cw_ref, z_ref, buf_ref,
                    *, tile, tiles_per_seq):
    i = pl.program_id(1)

    @pl.when(i % tiles_per_seq == 0)
    def _():
        buf_ref[0:HALO_A, :] = jnp.zeros((HALO_A, buf_ref.shape[1]), jnp.float32)

    h = h_ref[...]
    cu = _dot(h, wc_ref[...]) * _dot(h, wu_ref[...])
    buf_ref[HALO_A:HALO_A + tile, :] = cu
    conv = cw_ref[CONV_A - 1:CONV_A, :] * cu
    for k in range(CONV_A - 1):
        off = HALO_A - (CONV_A - 1) + k
        conv = conv + cw_ref[k:k + 1, :] * buf_ref[off:off + tile, :]
    buf_ref[0:HALO_A, :] = buf_ref[tile:tile + HALO_A, :]
    y = _dot(h, wb_ref[...]) * conv
    z_ref[...] = (_silu(_dot(h, wz_ref[...])) * y).astype(z_ref.dtype)


def _mixer_a_call(h, w_in, conv_w, layer, tile, chunk, tiles_per_seq):
    n = h.shape[0]
    n_chunks = WIDTH // chunk

    def wspec(k):
        base = (OFF_A + k * WIDTH) // chunk
        return pl.BlockSpec((None, D_MODEL, chunk), lambda j, i: (layer, 0, base + j))

    return pl.pallas_call(
        functools.partial(_mixer_a_kernel, tile=tile, tiles_per_seq=tiles_per_seq),
        grid=(n_chunks, n // tile),
        in_specs=[pl.BlockSpec((tile, D_MODEL), lambda j, i: (i, 0)),
                  wspec(0), wspec(1), wspec(2), wspec(3),
                  pl.BlockSpec((None, CONV_A, chunk), lambda j, i: (layer, 0, j))],
        out_specs=pl.BlockSpec((tile, chunk), lambda j, i: (i, j)),
        out_shape=jax.ShapeDtypeStruct((n, WIDTH), jnp.bfloat16),
        scratch_shapes=[pltpu.VMEM((tile + HALO_A, chunk), jnp.float32)],
        compiler_params=_params(("arbitrary", "arbitrary")),
        name="mixer_a",
    )(h, w_in, w_in, w_in, w_in, conv_w)


def _mixer_b_kernel(h_ref, wv_ref, wg_ref, wz_ref, cw_ref, cb_ref, lg_ref, lb_ref,
                    z_ref, v_ref, o_ref, *, tile, tiles_per_seq):
    i = pl.program_id(0)
    groups = WIDTH // LANES

    @pl.when(i % tiles_per_seq == 0)
    def _():
        v_ref[0:HALO_B * groups, :] = jnp.zeros((HALO_B * groups, LANES), jnp.float32)

    h = h_ref[...]
    v = _dot(h, wv_ref[...]) * jax.nn.sigmoid(_dot(h, wg_ref[...]))
    for g in range(groups):
        v_ref[pl.ds(HALO_B * groups + g, tile, stride=groups), :] = (
            v[:, g * LANES:(g + 1) * LANES])

    taps = [cw_ref[k * groups:(k + 1) * groups, :] for k in range(CONV_B)]
    unroll = 8

    def body(tb, carry):
        for r in range(unroll):
            t = tb * unroll + r
            acc = None
            for k in range(CONV_B):
                row = pl.multiple_of((t + HALO_B - (CONV_B - 1) + k) * groups, groups)
                term = taps[k] * v_ref[pl.ds(row, groups), :]
                acc = term if acc is None else acc + term
            o_ref[pl.ds(pl.multiple_of(t * groups, groups), groups), :] = acc
        return carry

    lax.fori_loop(0, tile // unroll, body, 0)
    v_ref[0:HALO_B * groups, :] = v_ref[tile * groups:(tile + HALO_B) * groups, :]

    cv = jnp.concatenate(
        [o_ref[pl.ds(g, tile, stride=groups), :] for g in range(groups)], axis=1)
    cv = cv + cb_ref[...]
    mu = jnp.mean(cv, axis=-1, keepdims=True)
    cen = cv - mu
    var = jnp.mean(cen * cen, axis=-1, keepdims=True)
    ln = cen * lax.rsqrt(var + LN_EPS) * lg_ref[...] + lb_ref[...]
    z_ref[...] = (_silu(_dot(h, wz_ref[...])) * _silu(ln)).astype(z_ref.dtype)


def _mixer_b_call(h, w_in, conv_w3, conv_b, ln_g, ln_b, layer, tile, tiles_per_seq):
    n = h.shape[0]
    groups = WIDTH // LANES

    def wspec(k):
        return pl.BlockSpec((None, D_MODEL, WIDTH),
                            lambda i: (layer, 0, OFF_B // WIDTH + k),
                            pipeline_mode=pl.Buffered(1))

    def vec():
        return pl.BlockSpec((None, 1, WIDTH), lambda i: (layer, 0, 0))

    return pl.pallas_call(
        functools.partial(_mixer_b_kernel, tile=tile, tiles_per_seq=tiles_per_seq),
        grid=(n // tile,),
        in_specs=[pl.BlockSpec((tile, D_MODEL), lambda i: (i, 0)),
                  wspec(0), wspec(1), wspec(2),
                  pl.BlockSpec((None, CONV_B * groups, LANES), lambda i: (layer, 0, 0)),
                  vec(), vec(), vec()],
        out_specs=pl.BlockSpec((tile, WIDTH), lambda i: (i, 0)),
        out_shape=jax.ShapeDtypeStruct((n, WIDTH), jnp.bfloat16),
        scratch_shapes=[pltpu.VMEM(((tile + HALO_B) * groups, LANES), jnp.float32),
                        pltpu.VMEM((tile * groups, LANES), jnp.float32)],
        compiler_params=_params(("arbitrary",)),
        name="mixer_b",
    )(h, w_in, w_in, w_in, conv_w3, conv_b, ln_g, ln_b)


def _mixer_c_kernel(h_ref, wu_ref, wz_ref, gw_ref, sc_ref, z_ref, buf_ref,
                    *, tile, tiles_per_seq):
    i = pl.program_id(0)
    seq_tile = i % tiles_per_seq

    @pl.when(seq_tile == 0)
    def _():
        buf_ref[0:HALO_C, :] = jnp.zeros((HALO_C, WIDTH), jnp.float32)

    h = h_ref[...]
    pos = (seq_tile * tile + 1
           + lax.broadcasted_iota(jnp.int32, (tile, C_GROUP), 0)).astype(jnp.float32)
    for g, win in enumerate(POOL_WINDOWS):
        cols = slice(g * C_GROUP, (g + 1) * C_GROUP)
        u = _dot(h, wu_ref[:, cols])
        buf_ref[HALO_C:HALO_C + tile, cols] = u
        total = u
        for d in range(1, win):
            total = total + buf_ref[HALO_C - d:HALO_C - d + tile, cols]
        pooled = total / jnp.minimum(pos, float(win))
        y = _dot((pooled - u).astype(jnp.bfloat16), gw_ref[g]) * sc_ref[:, cols]
        z_ref[:, cols] = (_silu(_dot(h, wz_ref[:, cols])) * y).astype(z_ref.dtype)
    buf_ref[0:HALO_C, :] = buf_ref[tile:tile + HALO_C, :]


def _mixer_c_call(h, w_in, group_w, scale, layer, tile, tiles_per_seq):
    n = h.shape[0]

    def wspec(k):
        return pl.BlockSpec((None, D_MODEL, WIDTH),
                            lambda i: (layer, 0, OFF_C // WIDTH + k),
                            pipeline_mode=pl.Buffered(1))

    return pl.pallas_call(
        functools.partial(_mixer_c_kernel, tile=tile, tiles_per_seq=tiles_per_seq),
        grid=(n // tile,),
        in_specs=[pl.BlockSpec((tile, D_MODEL), lambda i: (i, 0)),
                  wspec(0), wspec(1),
                  pl.BlockSpec((None, len(POOL_WINDOWS), C_GROUP, C_GROUP),
                               lambda i: (layer, 0, 0, 0)),
                  pl.BlockSpec((None, 1, WIDTH), lambda i: (layer, 0, 0))],
        out_specs=pl.BlockSpec((tile, WIDTH), lambda i: (i, 0)),
        out_shape=jax.ShapeDtypeStruct((n, WIDTH), jnp.bfloat16),
        scratch_shapes=[pltpu.VMEM((tile + HALO_C, WIDTH), jnp.float32)],
        compiler_params=_params(("arbitrary",)),
        name="mixer_c",
    )(h, w_in, w_in, group_w, scale)


def _merge_kernel(h_ref, za_ref, zb_ref, zc_ref, ga_ref, gb_ref, gc_ref,
                  wa_ref, wb_ref, wc_ref, m_ref):
    h = h_ref[...]
    m = jax.nn.sigmoid(_dot(h, ga_ref[...])) * _dot(za_ref[...], wa_ref[...])
    m = m + jax.nn.sigmoid(_dot(h, gb_ref[...])) * _dot(zb_ref[...], wb_ref[...])
    m = m + jax.nn.sigmoid(_dot(h, gc_ref[...])) * _dot(zc_ref[...], wc_ref[...])
    m_ref[...] = m.astype(m_ref.dtype)


def _merge_call(h, za, zb, zc, w_in, w_br_a, w_br_b, w_br_c, layer, tile, chunk):
    n = h.shape[0]

    def gspec(k):
        base = (OFF_G + k * D_MODEL) // chunk
        return pl.BlockSpec((None, D_MODEL, chunk), lambda c, i: (layer, 0, base + c))

    def bspec():
        return pl.BlockSpec((None, WIDTH, chunk), lambda c, i: (layer, 0, c))

    def zspec():
        return pl.BlockSpec((tile, WIDTH), lambda c, i: (i, 0))

    return pl.pallas_call(
        _merge_kernel,
        grid=(D_MODEL // chunk, n // tile),
        in_specs=[pl.BlockSpec((tile, D_MODEL), lambda c, i: (i, 0)),
                  zspec(), zspec(), zspec(),
                  gspec(0), gspec(1), gspec(2),
                  bspec(), bspec(), bspec()],
        out_specs=pl.BlockSpec((tile, chunk), lambda c, i: (i, c)),
        out_shape=jax.ShapeDtypeStruct((n, D_MODEL), jnp.bfloat16),
        compiler_params=_params(("arbitrary", "arbitrary")),
        name="merge",
    )(h, za, zb, zc, w_in, w_in, w_in, w_br_a, w_br_b, w_br_c)


def _out_kernel(x_ref, m_ref, wo_ref, g_ref, *out_refs, emit_x):
    x = x_ref[...] + _dot(m_ref[...], wo_ref[...])
    if emit_x:
        out_refs[0][...] = x
    hn_ref = out_refs[-1]
    r = x * lax.rsqrt(jnp.mean(x * x, axis=-1, keepdims=True) + RMS_EPS)
    hn_ref[...] = (r * g_ref[...]).astype(hn_ref.dtype)


def _out_call(x2, merged, w_o, g, layer, tile, emit_x, hn_dtype):
    n = x2.shape[0]
    row = pl.BlockSpec((tile, D_MODEL), lambda i: (i, 0))
    out_shape = [jax.ShapeDtypeStruct((n, D_MODEL), hn_dtype)]
    out_specs = [row]
    if emit_x:
        out_shape = [jax.ShapeDtypeStruct((n, D_MODEL), jnp.float32)] + out_shape
        out_specs = [row] + out_specs
    return pl.pallas_call(
        functools.partial(_out_kernel, emit_x=emit_x),
        grid=(n // tile,),
        in_specs=[row, row,
                  pl.BlockSpec((None, D_MODEL, D_MODEL), lambda i: (layer, 0, 0),
                               pipeline_mode=pl.Buffered(1)),
                  pl.BlockSpec((1, D_MODEL), lambda i: (0, 0))],
        out_specs=out_specs,
        out_shape=out_shape,
        compiler_params=_params(("arbitrary",)),
        name="out_proj",
    )(x2, merged, w_o, g)


def kernel(x, norm_g, w_in, a_conv_w, b_conv_w, b_conv_b, b_ln_g, b_ln_b, c_group_w,
           c_scale, w_br_a, w_br_b, w_br_c, w_o, final_g):
    bsz, seq, _ = x.shape
    depth = w_in.shape[0]
    n = bsz * seq
    tile = 512
    tile_b = 256
    assert seq % tile == 0 and seq % tile_b == 0
    groups = WIDTH // LANES
    bf16 = jnp.bfloat16

    w_in_h = w_in.astype(bf16)
    w_br_a_h, w_br_b_h, w_br_c_h = (w.astype(bf16) for w in (w_br_a, w_br_b, w_br_c))
    w_o_h = w_o.astype(bf16)
    group_w_h = c_group_w.astype(bf16)
    conv_w3 = b_conv_w.reshape(depth, CONV_B * groups, LANES)
    vec3 = lambda a: a.reshape(depth, 1, WIDTH)

    x2 = x.reshape(n, D_MODEL)
    h = _rmsnorm_call(x2, norm_g[0:1], 1024, bf16)
    for l in range(depth):
        za = _mixer_a_call(h, w_in_h, a_conv_w, l, tile, 256, seq // tile)
        zb = _mixer_b_call(h, w_in_h, conv_w3, vec3(b_conv_b), vec3(b_ln_g),
                           vec3(b_ln_b), l, tile_b, seq // tile_b)
        zc = _mixer_c_call(h, w_in_h, group_w_h, vec3(c_scale), l, tile, seq // tile)
        merged = _merge_call(h, za, zb, zc, w_in_h, w_br_a_h, w_br_b_h, w_br_c_h,
                             l, tile, 512)
        last = l == depth - 1
        if last:
            (out,) = _out_call(x2, merged, w_o_h, final_g.reshape(1, D_MODEL), l, tile,
                               False, x.dtype)
        else:
            x2, h = _out_call(x2, merged, w_o_h, norm_g[l + 1:l + 2], l, tile, True, bf16)
    return out.reshape(bsz, seq, D_MODEL)
```

```python
import functools

import jax
import jax.numpy as jnp
from jax import lax
from jax.experimental import pallas as pl
from jax.experimental.pallas import tpu as pltpu

D_MODEL = 2048
WIDTH = 1024
CONV_A = 3
CONV_B = 31
POOL_WINDOWS = (2, 4, 8, 16)
C_GROUP = WIDTH // len(POOL_WINDOWS)
RMS_EPS = 1e-6
LN_EPS = 1e-5

OFF_A = 0
OFF_B = 4 * WIDTH
OFF_C = 7 * WIDTH
OFF_G = 9 * WIDTH

SUBLANES = 8
LANES = 128
HALO_B = 32
HALO_C = 8
HALO_A = 8
MXU_COLS = 256

VMEM_LIMIT = 56 * 1024 * 1024


def _params(sem, flags=None):
    return pltpu.CompilerParams(dimension_semantics=sem, vmem_limit_bytes=VMEM_LIMIT,
                                flags=flags)


def _dot(a, b):
    return jnp.dot(a, b, preferred_element_type=jnp.float32)


def _silu(x):
    return x * jax.nn.sigmoid(x)


def _rms_kernel(x_ref, g_ref, h_ref):
    x = x_ref[...]
    r = x * lax.rsqrt(jnp.mean(x * x, axis=-1, keepdims=True) + RMS_EPS)
    h_ref[...] = (r * g_ref[...]).astype(h_ref.dtype)


def _rmsnorm_call(x2, g, tile, out_dtype):
    n = x2.shape[0]
    return pl.pallas_call(
        _rms_kernel,
        grid=(n // tile,),
        in_specs=[pl.BlockSpec((tile, D_MODEL), lambda i: (i, 0)),
                  pl.BlockSpec((1, D_MODEL), lambda i: (0, 0))],
        out_specs=pl.BlockSpec((tile, D_MODEL), lambda i: (i, 0)),
        out_shape=jax.ShapeDtypeStruct((n, D_MODEL), out_dtype),
        compiler_params=_params(("arbitrary",)),
        name="rmsnorm",
    )(x2, g)


def _mixer_a_kernel(h_ref, wu_ref, wb_ref, wc_ref, wz_ref, cw_ref, z_ref, buf_ref,
                    *, tile, tiles_per_seq):
    i = pl.program_id(1)

    @pl.when(i % tiles_per_seq == 0)
    def _():
        buf_ref[0:HALO_A, :] = jnp.zeros((HALO_A, buf_ref.shape[1]), jnp.float32)

    h = h_ref[...]
    cu = _dot(h, wc_ref[...]) * _dot(h, wu_ref[...])
    buf_ref[HALO_A:HALO_A + tile, :] = cu
    conv = cw_ref[CONV_A - 1:CONV_A, :] * cu
    for k in range(CONV_A - 1):
        off = HALO_A - (CONV_A - 1) + k
        conv = conv + cw_ref[k:k + 1, :] * buf_ref[off:off + tile, :]
    buf_ref[0:HALO_A, :] = buf_ref[tile:tile + HALO_A, :]
    y = _dot(h, wb_ref[...]) * conv
    z_ref[...] = (_silu(_dot(h, wz_ref[...])) * y).astype(z_ref.dtype)


def _mixer_a_call(h, w_in, conv_w, layer, tile, chunk, tiles_per_seq):
    n = h.shape[0]
    n_chunks = WIDTH // chunk

    def wspec(k):
        base = (OFF_A + k * WIDTH) // chunk
        return pl.BlockSpec((None, D_MODEL, chunk), lambda j, i: (layer, 0, base + j))

    return pl.pallas_call(
        functools.partial(_mixer_a_kernel, tile=tile, tiles_per_seq=tiles_per_seq),
        grid=(n_chunks, n // tile),
        in_specs=[pl.BlockSpec((tile, D_MODEL), lambda j, i: (i, 0)),
                  wspec(0), wspec(1), wspec(2), wspec(3),
                  pl.BlockSpec((None, CONV_A, chunk), lambda j, i: (layer, 0, j))],
        out_specs=pl.BlockSpec((tile, chunk), lambda j, i: (i, j)),
        out_shape=jax.ShapeDtypeStruct((n, WIDTH), jnp.bfloat16),
        scratch_shapes=[pltpu.VMEM((tile + HALO_A, chunk), jnp.float32)],
        compiler_params=_params(("arbitrary", "arbitrary")),
        name="mixer_a",
    )(h, w_in, w_in, w_in, w_in, conv_w)


def _mixer_b_project(kind, c, h_ref, wv_ref, wg_ref, wz_ref, pv_ref, vb_w, pz_w, *, tile):
    groups = WIDTH // LANES
    per = MXU_COLS // LANES
    cols = slice(c * MXU_COLS, (c + 1) * MXU_COLS)
    if kind == "v":
        pv_ref[...] = _dot(h_ref[...], wv_ref[:, cols])
    elif kind == "g":
        v = pv_ref[...] * jax.nn.sigmoid(_dot(h_ref[...], wg_ref[:, cols]))
        for p in range(per):
            vb_w[pl.ds(HALO_B * groups + c * per + p, tile, stride=groups), :] = (
                v[:, p * LANES:(p + 1) * LANES])
    else:
        pz_w[:, cols] = _dot(h_ref[...], wz_ref[:, cols])


def _mixer_b_finish(rows, vb_r, pz_r, o_ref, cw_ref, cb_ref, lg_ref, lb_ref, z_ref):
    groups = WIDTH // LANES
    taps = [cw_ref[k * groups:(k + 1) * groups, :] for k in range(CONV_B)]
    for t in range(rows.start, rows.stop):
        acc = None
        for k in range(CONV_B):
            row = (t + HALO_B - (CONV_B - 1) + k) * groups
            term = taps[k] * vb_r[row:row + groups, :]
            acc = term if acc is None else acc + term
        o_ref[t * groups:(t + 1) * groups, :] = acc
    n_rows = rows.stop - rows.start
    cv = jnp.concatenate(
        [o_ref[pl.ds(rows.start * groups + g, n_rows, stride=groups), :]
         for g in range(groups)], axis=1)
    cv = cv + cb_ref[...]
    mu = jnp.mean(cv, axis=-1, keepdims=True)
    cen = cv - mu
    var = jnp.mean(cen * cen, axis=-1, keepdims=True)
    ln = cen * lax.rsqrt(var + LN_EPS) * lg_ref[...] + lb_ref[...]
    z_ref[rows, :] = (_silu(pz_r[rows, :]) * _silu(ln)).astype(z_ref.dtype)


def _mixer_b_kernel(h_ref, wv_ref, wg_ref, wz_ref, cw_ref, cb_ref, lg_ref, lb_ref,
                    one_ref, z_ref, vb_ref, pz_ref, pv_ref, o_ref, *, tile, tiles_per_seq):
    step = pl.program_id(0)
    groups = WIDTH // LANES
    n_chunks = WIDTH // MXU_COLS
    regions = [[("v", c), ("g", c), ("z", c)] for c in range(n_chunks)]
    blocks = tile // SUBLANES
    bounds = [SUBLANES * ((blocks * j) // len(regions)) for j in range(len(regions) + 1)]

    @pl.when(step == 0)
    def _():
        vb_ref[1] = jnp.zeros(vb_ref.shape[1:], jnp.float32)
        pz_ref[1] = jnp.zeros(pz_ref.shape[1:], jnp.float32)

    def run(slot_w, slot_r):
        vb_w, vb_r = vb_ref.at[slot_w], vb_ref.at[slot_r]
        hist = vb_r[tile * groups:(tile + HALO_B) * groups, :]
        vb_w[0:HALO_B * groups, :] = jnp.where(step % tiles_per_seq == 0, 0.0, hist)
        for j, dots in enumerate(regions):
            @pl.when(one_ref[0] != 0)
            def _(j=j, dots=dots):
                for kind, c in dots:
                    _mixer_b_project(kind, c, h_ref, wv_ref, wg_ref, wz_ref, pv_ref, vb_w,
                                     pz_ref.at[slot_w], tile=tile)
                _mixer_b_finish(slice(bounds[j], bounds[j + 1]), vb_r, pz_ref.at[slot_r],
                                o_ref, cw_ref, cb_ref, lg_ref, lb_ref, z_ref)

    @pl.when(step % 2 == 0)
    def _():
        run(0, 1)

    @pl.when(step % 2 == 1)
    def _():
        run(1, 0)


def _mixer_b_call(h, w_in, conv_w3, conv_b, ln_g, ln_b, layer, tile, tiles_per_seq):
    n = h.shape[0]
    n_tiles = n // tile
    groups = WIDTH // LANES

    def wspec(k):
        return pl.BlockSpec((None, D_MODEL, WIDTH),
                            lambda s: (layer, 0, OFF_B // WIDTH + k),
                            pipeline_mode=pl.Buffered(1))

    def vec():
        return pl.BlockSpec((None, 1, WIDTH), lambda s: (layer, 0, 0))

    return pl.pallas_call(
        functools.partial(_mixer_b_kernel, tile=tile, tiles_per_seq=tiles_per_seq),
        grid=(n_tiles + 1,),
        in_specs=[pl.BlockSpec((tile, D_MODEL),
                               lambda s: (jnp.minimum(s, n_tiles - 1), 0)),
                  wspec(0), wspec(1), wspec(2),
                  pl.BlockSpec((None, CONV_B * groups, LANES), lambda s: (layer, 0, 0)),
                  vec(), vec(), vec(),
                  pl.BlockSpec(memory_space=pltpu.SMEM)],
        out_specs=pl.BlockSpec((tile, WIDTH), lambda s: (jnp.maximum(s - 1, 0), 0)),
        out_shape=jax.ShapeDtypeStruct((n, WIDTH), jnp.bfloat16),
        scratch_shapes=[pltpu.VMEM((2, (tile + HALO_B) * groups, LANES), jnp.float32),
                        pltpu.VMEM((2, tile, WIDTH), jnp.float32),
                        pltpu.VMEM((tile, MXU_COLS), jnp.float32),
                        pltpu.VMEM((tile * groups, LANES), jnp.float32)],
        compiler_params=_params(("arbitrary",)),
        name="mixer_b",
    )(h, w_in, w_in, w_in, conv_w3, conv_b, ln_g, ln_b, jnp.ones((1,), jnp.int32))


def _mixer_c_kernel(h_ref, wu_ref, wz_ref, gw_ref, sc_ref, z_ref, buf_ref,
                    *, tile, tiles_per_seq):
    i = pl.program_id(0)
    seq_tile = i % tiles_per_seq

    n_stage = buf_ref.shape[0]

    @pl.when(seq_tile == 0)
    def _():
        buf_ref[:, 0:HALO_C, :] = jnp.zeros((n_stage, HALO_C, WIDTH), jnp.float32)

    h = h_ref[...]
    head_pos = (seq_tile * tile + 1
                + lax.broadcasted_iota(jnp.int32, (HALO_C * 2, C_GROUP), 0)
                ).astype(jnp.float32)
    for g, win in enumerate(POOL_WINDOWS):
        stages = win.bit_length() - 1
        assert win == 1 << stages and win <= 2 * HALO_C and stages <= n_stage
        cols = slice(g * C_GROUP, (g + 1) * C_GROUP)
        u = _dot(h, wu_ref[:, cols])
        total = u
        for k in range(stages):
            shift = 1 << k
            buf_ref[k, HALO_C:HALO_C + tile, cols] = total
            total = total + buf_ref[k, HALO_C - shift:HALO_C - shift + tile, cols]
            buf_ref[k, 0:HALO_C, cols] = buf_ref[k, tile:tile + HALO_C, cols]
        inv_cnt = jnp.concatenate(
            [1.0 / jnp.minimum(head_pos, float(win)),
             jnp.full((tile - HALO_C * 2, C_GROUP), 1.0 / win, jnp.float32)], axis=0)
        pooled = total * inv_cnt
        y = _dot((pooled - u).astype(jnp.bfloat16), gw_ref[g]) * sc_ref[:, cols]
        z_ref[:, cols] = (_silu(_dot(h, wz_ref[:, cols])) * y).astype(z_ref.dtype)


def _mixer_c_call(h, w_in, group_w, scale, layer, tile, tiles_per_seq):
    n = h.shape[0]

    def wspec(k):
        return pl.BlockSpec((None, D_MODEL, WIDTH),
                            lambda i: (layer, 0, OFF_C // WIDTH + k),
                            pipeline_mode=pl.Buffered(1))

    return pl.pallas_call(
        functools.partial(_mixer_c_kernel, tile=tile, tiles_per_seq=tiles_per_seq),
        grid=(n // tile,),
        in_specs=[pl.BlockSpec((tile, D_MODEL), lambda i: (i, 0)),
                  wspec(0), wspec(1),
                  pl.BlockSpec((None, len(POOL_WINDOWS), C_GROUP, C_GROUP),
                               lambda i: (layer, 0, 0, 0)),
                  pl.BlockSpec((None, 1, WIDTH), lambda i: (layer, 0, 0))],
        out_specs=pl.BlockSpec((tile, WIDTH), lambda i: (i, 0)),
        out_shape=jax.ShapeDtypeStruct((n, WIDTH), jnp.bfloat16),
        scratch_shapes=[pltpu.VMEM((max(POOL_WINDOWS).bit_length() - 1, tile + HALO_C, WIDTH),
                                   jnp.float32)],
        compiler_params=_params(("arbitrary",)),
        name="mixer_c",
    )(h, w_in, w_in, group_w, scale)


def _merge_kernel(h_ref, za_ref, zb_ref, zc_ref, ga_ref, gb_ref, gc_ref,
                  wa_ref, wb_ref, wc_ref, m_ref):
    h = h_ref[...]
    m = jax.nn.sigmoid(_dot(h, ga_ref[...])) * _dot(za_ref[...], wa_ref[...])
    m = m + jax.nn.sigmoid(_dot(h, gb_ref[...])) * _dot(zb_ref[...], wb_ref[...])
    m = m + jax.nn.sigmoid(_dot(h, gc_ref[...])) * _dot(zc_ref[...], wc_ref[...])
    m_ref[...] = m.astype(m_ref.dtype)


def _merge_call(h, za, zb, zc, w_in, w_br_a, w_br_b, w_br_c, layer, tile, chunk):
    n = h.shape[0]

    def gspec(k):
        base = (OFF_G + k * D_MODEL) // chunk
        return pl.BlockSpec((None, D_MODEL, chunk), lambda c, i: (layer, 0, base + c))

    def bspec():
        return pl.BlockSpec((None, WIDTH, chunk), lambda c, i: (layer, 0, c))

    def zspec():
        return pl.BlockSpec((tile, WIDTH), lambda c, i: (i, 0))

    return pl.pallas_call(
        _merge_kernel,
        grid=(D_MODEL // chunk, n // tile),
        in_specs=[pl.BlockSpec((tile, D_MODEL), lambda c, i: (i, 0)),
                  zspec(), zspec(), zspec(),
                  gspec(0), gspec(1), gspec(2),
                  bspec(), bspec(), bspec()],
        out_specs=pl.BlockSpec((tile, chunk), lambda c, i: (i, c)),
        out_shape=jax.ShapeDtypeStruct((n, D_MODEL), jnp.bfloat16),
        compiler_params=_params(("arbitrary", "arbitrary")),
        name="merge",
    )(h, za, zb, zc, w_in, w_in, w_in, w_br_a, w_br_b, w_br_c)


def _out_kernel(x_ref, m_ref, wo_ref, g_ref, *out_refs, emit_x):
    x = x_ref[...] + _dot(m_ref[...], wo_ref[...])
    if emit_x:
        out_refs[0][...] = x
    hn_ref = out_refs[-1]
    r = x * lax.rsqrt(jnp.mean(x * x, axis=-1, keepdims=True) + RMS_EPS)
    hn_ref[...] = (r * g_ref[...]).astype(hn_ref.dtype)


def _out_call(x2, merged, w_o, g, layer, tile, emit_x, hn_dtype):
    n = x2.shape[0]
    row = pl.BlockSpec((tile, D_MODEL), lambda i: (i, 0))
    out_shape = [jax.ShapeDtypeStruct((n, D_MODEL), hn_dtype)]
    out_specs = [row]
    if emit_x:
        out_shape = [jax.ShapeDtypeStruct((n, D_MODEL), jnp.float32)] + out_shape
        out_specs = [row] + out_specs
    return pl.pallas_call(
        functools.partial(_out_kernel, emit_x=emit_x),
        grid=(n // tile,),
        in_specs=[row, row,
                  pl.BlockSpec((None, D_MODEL, D_MODEL), lambda i: (layer, 0, 0),
                               pipeline_mode=pl.Buffered(1)),
                  pl.BlockSpec((1, D_MODEL), lambda i: (0, 0))],
        out_specs=out_specs,
        out_shape=out_shape,
        compiler_params=_params(("arbitrary",)),
        name="out_proj",
    )(x2, merged, w_o, g)


def kernel(x, norm_g, w_in, a_conv_w, b_conv_w, b_conv_b, b_ln_g, b_ln_b, c_group_w,
           c_scale, w_br_a, w_br_b, w_br_c, w_o, final_g):
    bsz, seq, _ = x.shape
    depth = w_in.shape[0]
    n = bsz * seq
    tile = 512
    tile_b = 256
    assert seq % tile == 0 and seq % tile_b == 0
    groups = WIDTH // LANES
    bf16 = jnp.bfloat16

    w_in_h = w_in.astype(bf16)
    w_br_a_h, w_br_b_h, w_br_c_h = (w.astype(bf16) for w in (w_br_a, w_br_b, w_br_c))
    w_o_h = w_o.astype(bf16)
    group_w_h = c_group_w.astype(bf16)
    conv_w3 = b_conv_w.reshape(depth, CONV_B * groups, LANES)
    vec3 = lambda a: a.reshape(depth, 1, WIDTH)

    x2 = x.reshape(n, D_MODEL)
    h = _rmsnorm_call(x2, norm_g[0:1], 1024, bf16)
    for l in range(depth):
        za = _mixer_a_call(h, w_in_h, a_conv_w, l, tile, 256, seq // tile)
        zb = _mixer_b_call(h, w_in_h, conv_w3, vec3(b_conv_b), vec3(b_ln_g),
                           vec3(b_ln_b), l, tile_b, seq // tile_b)
        zc = _mixer_c_call(h, w_in_h, group_w_h, vec3(c_scale), l, tile, seq // tile)
        merged = _merge_call(h, za, zb, zc, w_in_h, w_br_a_h, w_br_b_h, w_br_c_h,
                             l, tile, 512)
        last = l == depth - 1
        if last:
            (out,) = _out_call(x2, merged, w_o_h, final_g.reshape(1, D_MODEL), l, tile,
                               False, x.dtype)
        else:
            x2, h = _out_call(x2, merged, w_o_h, norm_g[l + 1:l + 2], l, tile, True, bf16)
    return out.reshape(bsz, seq, D_MODEL)
```

```python
import functools

import jax
import jax.numpy as jnp
from jax import lax
from jax.experimental import pallas as pl
from jax.experimental.pallas import tpu as pltpu

D_MODEL = 2048
WIDTH = 1024
CONV_A = 3
CONV_B = 31
POOL_WINDOWS = (2, 4, 8, 16)
C_GROUP = WIDTH // len(POOL_WINDOWS)
RMS_EPS = 1e-6
LN_EPS = 1e-5

OFF_A = 0
OFF_B = 4 * WIDTH
OFF_C = 7 * WIDTH
OFF_G = 9 * WIDTH

SUBLANES = 8
LANES = 128
HALO_B = 32
HALO_C = 8
HALO_A = 8
MXU_COLS = 256

VMEM_LIMIT = 56 * 1024 * 1024


def _params(sem):
    return pltpu.CompilerParams(dimension_semantics=sem, vmem_limit_bytes=VMEM_LIMIT)


def _dot(a, b):
    return jnp.dot(a, b, preferred_element_type=jnp.float32)


def _silu(x):
    return x * jax.nn.sigmoid(x)


def _rms_kernel(x_ref, g_ref, h_ref):
    x = x_ref[...]
    r = x * lax.rsqrt(jnp.mean(x * x, axis=-1, keepdims=True) + RMS_EPS)
    h_ref[...] = (r * g_ref[...]).astype(h_ref.dtype)


def _rmsnorm_call(x2, g, tile, out_dtype):
    n = x2.shape[0]
    return pl.pallas_call(
        _rms_kernel,
        grid=(n // tile,),
        in_specs=[pl.BlockSpec((tile, D_MODEL), lambda i: (i, 0)),
                  pl.BlockSpec((1, D_MODEL), lambda i: (0, 0))],
        out_specs=pl.BlockSpec((tile, D_MODEL), lambda i: (i, 0)),
        out_shape=jax.ShapeDtypeStruct((n, D_MODEL), out_dtype),
        compiler_params=_params(("arbitrary",)),
        name="rmsnorm",
    )(x2, g)


def _mixer_a_kernel(h_ref, wu_ref, wb_ref, wc_ref, wz_ref, cw_ref, z_ref, buf_ref,
                    *, tile, tiles_per_seq):
    i = pl.program_id(0)

    @pl.when(i % tiles_per_seq == 0)
    def _():
        buf_ref[0:HALO_A, :] = jnp.zeros((HALO_A, WIDTH), jnp.float32)

    h = h_ref[...]
    for c in range(WIDTH // MXU_COLS):
        cols = slice(c * MXU_COLS, (c + 1) * MXU_COLS)
        cu = _dot(h, wc_ref[:, cols]) * _dot(h, wu_ref[:, cols])
        buf_ref[HALO_A:HALO_A + tile, cols] = cu
        conv = cw_ref[CONV_A - 1:CONV_A, cols] * cu
        for k in range(CONV_A - 1):
            off = HALO_A - (CONV_A - 1) + k
            conv = conv + cw_ref[k:k + 1, cols] * buf_ref[off:off + tile, cols]
        buf_ref[0:HALO_A, cols] = buf_ref[tile:tile + HALO_A, cols]
        y = _dot(h, wb_ref[:, cols]) * conv
        z_ref[:, cols] = (_silu(_dot(h, wz_ref[:, cols])) * y).astype(z_ref.dtype)


def _mixer_a_call(h, w_in, conv_w, layer, tile, tiles_per_seq):
    n = h.shape[0]

    def wspec(k):
        return pl.BlockSpec((None, D_MODEL, WIDTH),
                            lambda i: (layer, 0, OFF_A // WIDTH + k),
                            pipeline_mode=pl.Buffered(1))

    return pl.pallas_call(
        functools.partial(_mixer_a_kernel, tile=tile, tiles_per_seq=tiles_per_seq),
        grid=(n // tile,),
        in_specs=[pl.BlockSpec((tile, D_MODEL), lambda i: (i, 0)),
                  wspec(0), wspec(1), wspec(2), wspec(3),
                  pl.BlockSpec((None, CONV_A, WIDTH), lambda i: (layer, 0, 0))],
        out_specs=pl.BlockSpec((tile, WIDTH), lambda i: (i, 0)),
        out_shape=jax.ShapeDtypeStruct((n, WIDTH), jnp.bfloat16),
        scratch_shapes=[pltpu.VMEM((tile + HALO_A, WIDTH), jnp.float32)],
        compiler_params=_params(("arbitrary",)),
        name="mixer_a",
    )(h, w_in, w_in, w_in, w_in, conv_w)


def _mixer_b_kernel(h_ref, wv_ref, wg_ref, wz_ref, cw_ref, cb_ref, lg_ref, lb_ref,
                    z_ref, v_ref, o_ref, *, tile, tiles_per_seq):
    i = pl.program_id(0)
    groups = WIDTH // LANES

    @pl.when(i % tiles_per_seq == 0)
    def _():
        v_ref[0:HALO_B * groups, :] = jnp.zeros((HALO_B * groups, LANES), jnp.float32)

    h = h_ref[...]
    per = MXU_COLS // LANES
    for c in range(WIDTH // MXU_COLS):
        cols = slice(c * MXU_COLS, (c + 1) * MXU_COLS)
        v = _dot(h, wv_ref[:, cols]) * jax.nn.sigmoid(_dot(h, wg_ref[:, cols]))
        for p in range(per):
            v_ref[pl.ds(HALO_B * groups + c * per + p, tile, stride=groups), :] = (
                v[:, p * LANES:(p + 1) * LANES])

    taps = [cw_ref[k * groups:(k + 1) * groups, :] for k in range(CONV_B)]
    unroll = 8

    def body(tb, carry):
        for r in range(unroll):
            t = tb * unroll + r
            acc = None
            for k in range(CONV_B):
                row = pl.multiple_of((t + HALO_B - (CONV_B - 1) + k) * groups, groups)
                term = taps[k] * v_ref[pl.ds(row, groups), :]
                acc = term if acc is None else acc + term
            o_ref[pl.ds(pl.multiple_of(t * groups, groups), groups), :] = acc
        return carry

    lax.fori_loop(0, tile // unroll, body, 0)
    v_ref[0:HALO_B * groups, :] = v_ref[tile * groups:(tile + HALO_B) * groups, :]

    cv = jnp.concatenate(
        [o_ref[pl.ds(g, tile, stride=groups), :] for g in range(groups)], axis=1)
    cv = cv + cb_ref[...]
    mu = jnp.mean(cv, axis=-1, keepdims=True)
    cen = cv - mu
    var = jnp.mean(cen * cen, axis=-1, keepdims=True)
    rstd = lax.rsqrt(var + LN_EPS)
    for c in range(WIDTH // MXU_COLS):
        cols = slice(c * MXU_COLS, (c + 1) * MXU_COLS)
        ln = cen[:, cols] * rstd * lg_ref[:, cols] + lb_ref[:, cols]
        z_ref[:, cols] = (_silu(_dot(h, wz_ref[:, cols])) * _silu(ln)).astype(z_ref.dtype)


def _mixer_b_call(h, w_in, conv_w3, conv_b, ln_g, ln_b, layer, tile, tiles_per_seq):
    n = h.shape[0]
    groups = WIDTH // LANES

    def wspec(k):
        return pl.BlockSpec((None, D_MODEL, WIDTH),
                            lambda i: (layer, 0, OFF_B // WIDTH + k),
                            pipeline_mode=pl.Buffered(1))

    def vec():
        return pl.BlockSpec((None, 1, WIDTH), lambda i: (layer, 0, 0))

    return pl.pallas_call(
        functools.partial(_mixer_b_kernel, tile=tile, tiles_per_seq=tiles_per_seq),
        grid=(n // tile,),
        in_specs=[pl.BlockSpec((tile, D_MODEL), lambda i: (i, 0)),
                  wspec(0), wspec(1), wspec(2),
                  pl.BlockSpec((None, CONV_B * groups, LANES), lambda i: (layer, 0, 0)),
                  vec(), vec(), vec()],
        out_specs=pl.BlockSpec((tile, WIDTH), lambda i: (i, 0)),
        out_shape=jax.ShapeDtypeStruct((n, WIDTH), jnp.bfloat16),
        scratch_shapes=[pltpu.VMEM(((tile + HALO_B) * groups, LANES), jnp.float32),
                        pltpu.VMEM((tile * groups, LANES), jnp.float32)],
        compiler_params=_params(("arbitrary",)),
        name="mixer_b",
    )(h, w_in, w_in, w_in, conv_w3, conv_b, ln_g, ln_b)


def _mixer_c_kernel(h_ref, wu_ref, wz_ref, gw_ref, sc_ref, z_ref, buf_ref,
                    *, tile, tiles_per_seq):
    i = pl.program_id(0)
    seq_tile = i % tiles_per_seq
    n_stage = buf_ref.shape[0]

    @pl.when(seq_tile == 0)
    def _():
        buf_ref[:, 0:HALO_C, :] = jnp.zeros((n_stage, HALO_C, WIDTH), jnp.float32)

    h = h_ref[...]
    head_pos = (seq_tile * tile + 1
                + lax.broadcasted_iota(jnp.int32, (HALO_C * 2, C_GROUP), 0)
                ).astype(jnp.float32)
    for g, win in enumerate(POOL_WINDOWS):
        stages = win.bit_length() - 1
        assert win == 1 << stages and win <= 2 * HALO_C and stages <= n_stage
        cols = slice(g * C_GROUP, (g + 1) * C_GROUP)
        u = _dot(h, wu_ref[:, cols])
        total = u
        for k in range(stages):
            shift = 1 << k
            buf_ref[k, HALO_C:HALO_C + tile, cols] = total
            total = total + buf_ref[k, HALO_C - shift:HALO_C - shift + tile, cols]
            buf_ref[k, 0:HALO_C, cols] = buf_ref[k, tile:tile + HALO_C, cols]
        inv_cnt = jnp.concatenate(
            [1.0 / jnp.minimum(head_pos, float(win)),
             jnp.full((tile - HALO_C * 2, C_GROUP), 1.0 / win, jnp.float32)], axis=0)
        pooled = total * inv_cnt
        y = _dot((pooled - u).astype(jnp.bfloat16), gw_ref[g]) * sc_ref[:, cols]
        z_ref[:, cols] = (_silu(_dot(h, wz_ref[:, cols])) * y).astype(z_ref.dtype)


def _mixer_c_call(h, w_in, group_w, scale, layer, tile, tiles_per_seq):
    n = h.shape[0]

    def wspec(k):
        return pl.BlockSpec((None, D_MODEL, WIDTH),
                            lambda i: (layer, 0, OFF_C // WIDTH + k),
                            pipeline_mode=pl.Buffered(1))

    return pl.pallas_call(
        functools.partial(_mixer_c_kernel, tile=tile, tiles_per_seq=tiles_per_seq),
        grid=(n // tile,),
        in_specs=[pl.BlockSpec((tile, D_MODEL), lambda i: (i, 0)),
                  wspec(0), wspec(1),
                  pl.BlockSpec((None, len(POOL_WINDOWS), C_GROUP, C_GROUP),
                               lambda i: (layer, 0, 0, 0)),
                  pl.BlockSpec((None, 1, WIDTH), lambda i: (layer, 0, 0))],
        out_specs=pl.BlockSpec((tile, WIDTH), lambda i: (i, 0)),
        out_shape=jax.ShapeDtypeStruct((n, WIDTH), jnp.bfloat16),
        scratch_shapes=[pltpu.VMEM((max(POOL_WINDOWS).bit_length() - 1, tile + HALO_C, WIDTH),
                                   jnp.float32)],
        compiler_params=_params(("arbitrary",)),
        name="mixer_c",
    )(h, w_in, w_in, group_w, scale)


def _merge_kernel(h_ref, za_ref, zb_ref, zc_ref, ga_ref, gb_ref, gc_ref,
                  wa_ref, wb_ref, wc_ref, m_ref, *, sub):
    h = h_ref[...]
    for s in range(m_ref.shape[1] // sub):
        cols = slice(s * sub, (s + 1) * sub)
        m = jax.nn.sigmoid(_dot(h, ga_ref[:, cols])) * _dot(za_ref[...], wa_ref[:, cols])
        m = m + jax.nn.sigmoid(_dot(h, gb_ref[:, cols])) * _dot(zb_ref[...], wb_ref[:, cols])
        m = m + jax.nn.sigmoid(_dot(h, gc_ref[:, cols])) * _dot(zc_ref[...], wc_ref[:, cols])
        m_ref[:, cols] = m.astype(m_ref.dtype)


def _merge_call(h, za, zb, zc, w_in, w_br_a, w_br_b, w_br_c, layer, tile, chunk, sub):
    n = h.shape[0]

    def gspec(k):
        base = (OFF_G + k * D_MODEL) // chunk
        return pl.BlockSpec((None, D_MODEL, chunk), lambda c, i: (layer, 0, base + c),
                            pipeline_mode=pl.Buffered(1))

    def bspec():
        return pl.BlockSpec((None, WIDTH, chunk), lambda c, i: (layer, 0, c),
                            pipeline_mode=pl.Buffered(1))

    def zspec():
        return pl.BlockSpec((tile, WIDTH), lambda c, i: (i, 0))

    return pl.pallas_call(
        functools.partial(_merge_kernel, sub=sub),
        grid=(D_MODEL // chunk, n // tile),
        in_specs=[pl.BlockSpec((tile, D_MODEL), lambda c, i: (i, 0)),
                  zspec(), zspec(), zspec(),
                  gspec(0), gspec(1), gspec(2),
                  bspec(), bspec(), bspec()],
        out_specs=pl.BlockSpec((tile, chunk), lambda c, i: (i, c)),
        out_shape=jax.ShapeDtypeStruct((n, D_MODEL), jnp.bfloat16),
        compiler_params=_params(("arbitrary", "arbitrary")),
        name="merge",
    )(h, za, zb, zc, w_in, w_in, w_in, w_br_a, w_br_b, w_br_c)


def _out_kernel(x_ref, m_ref, wo_ref, g_ref, *out_refs, emit_x):
    x = x_ref[...] + _dot(m_ref[...], wo_ref[...])
    if emit_x:
        out_refs[0][...] = x
    hn_ref = out_refs[-1]
    r = x * lax.rsqrt(jnp.mean(x * x, axis=-1, keepdims=True) + RMS_EPS)
    hn_ref[...] = (r * g_ref[...]).astype(hn_ref.dtype)


def _out_call(x2, merged, w_o, g, layer, tile, emit_x, hn_dtype):
    n = x2.shape[0]
    row = pl.BlockSpec((tile, D_MODEL), lambda i: (i, 0))
    out_shape = [jax.ShapeDtypeStruct((n, D_MODEL), hn_dtype)]
    out_specs = [row]
    if emit_x:
        out_shape = [jax.ShapeDtypeStruct((n, D_MODEL), jnp.float32)] + out_shape
        out_specs = [row] + out_specs
    return pl.pallas_call(
        functools.partial(_out_kernel, emit_x=emit_x),
        grid=(n // tile,),
        in_specs=[row, row,
                  pl.BlockSpec((None, D_MODEL, D_MODEL), lambda i: (layer, 0, 0),
                               pipeline_mode=pl.Buffered(1)),
                  pl.BlockSpec((1, D_MODEL), lambda i: (0, 0))],
        out_specs=out_specs,
        out_shape=out_shape,
        compiler_params=_params(("arbitrary",)),
        name="out_proj",
    )(x2, merged, w_o, g)


def kernel(x, norm_g, w_in, a_conv_w, b_conv_w, b_conv_b, b_ln_g, b_ln_b, c_group_w,
           c_scale, w_br_a, w_br_b, w_br_c, w_o, final_g):
    bsz, seq, _ = x.shape
    depth = w_in.shape[0]
    n = bsz * seq
    tile = 512
    assert seq % tile == 0
    groups = WIDTH // LANES
    bf16 = jnp.bfloat16

    w_in_h = w_in.astype(bf16)
    w_br_a_h, w_br_b_h, w_br_c_h = (w.astype(bf16) for w in (w_br_a, w_br_b, w_br_c))
    w_o_h = w_o.astype(bf16)
    group_w_h = c_group_w.astype(bf16)
    conv_w3 = b_conv_w.reshape(depth, CONV_B * groups, LANES)
    vec3 = lambda a: a.reshape(depth, 1, WIDTH)

    x2 = x.reshape(n, D_MODEL)
    h = _rmsnorm_call(x2, norm_g[0:1], 1024, bf16)
    for l in range(depth):
        za = _mixer_a_call(h, w_in_h, a_conv_w, l, tile, seq // tile)
        zb = _mixer_b_call(h, w_in_h, conv_w3, vec3(b_conv_b), vec3(b_ln_g),
                           vec3(b_ln_b), l, tile, seq // tile)
        zc = _mixer_c_call(h, w_in_h, group_w_h, vec3(c_scale), l, tile, seq // tile)
        merged = _merge_call(h, za, zb, zc, w_in_h, w_br_a_h, w_br_b_h, w_br_c_h,
                             l, tile, 1024, 512)
        last = l == depth - 1
        if last:
            (out,) = _out_call(x2, merged, w_o_h, final_g.reshape(1, D_MODEL), l, tile,
                               False, x.dtype)
        else:
            x2, h = _out_call(x2, merged, w_o_h, norm_g[l + 1:l + 2], l, tile, True, bf16)
    return out.reshape(bsz, seq, D_MODEL)
```

```python
import functools

import jax
import jax.numpy as jnp
from jax import lax
from jax.experimental import pallas as pl
from jax.experimental.pallas import tpu as pltpu

D_MODEL = 2048
WIDTH = 1024
CONV_A = 3
CONV_B = 31
POOL_WINDOWS = (2, 4, 8, 16)
C_GROUP = WIDTH // len(POOL_WINDOWS)
RMS_EPS = 1e-6
LN_EPS = 1e-5

OFF_A = 0
OFF_B = 4 * WIDTH
OFF_C = 7 * WIDTH
OFF_G = 9 * WIDTH

SUBLANES = 8
LANES = 128
HALO_B = 32
HALO_C = 8
HALO_A = 8
MXU_COLS = 256

VMEM_LIMIT = 56 * 1024 * 1024


def _params(sem):
    return pltpu.CompilerParams(dimension_semantics=sem, vmem_limit_bytes=VMEM_LIMIT)


def _dot(a, b):
    return jnp.dot(a, b, preferred_element_type=jnp.float32)


def _silu(x):
    return x * jax.nn.sigmoid(x)


def _rms_kernel(x_ref, g_ref, h_ref):
    x = x_ref[...]
    r = x * lax.rsqrt(jnp.mean(x * x, axis=-1, keepdims=True) + RMS_EPS)
    h_ref[...] = (r * g_ref[...]).astype(h_ref.dtype)


def _rmsnorm_call(x2, g, tile, out_dtype):
    n = x2.shape[0]
    return pl.pallas_call(
        _rms_kernel,
        grid=(n // tile,),
        in_specs=[pl.BlockSpec((tile, D_MODEL), lambda i: (i, 0)),
                  pl.BlockSpec((1, D_MODEL), lambda i: (0, 0))],
        out_specs=pl.BlockSpec((tile, D_MODEL), lambda i: (i, 0)),
        out_shape=jax.ShapeDtypeStruct((n, D_MODEL), out_dtype),
        compiler_params=_params(("arbitrary",)),
        name="rmsnorm",
    )(x2, g)


def _mixer_a_kernel(h_ref, wu_ref, wb_ref, wc_ref, wz_ref, cw_ref, z_ref, buf_ref,
                    *, tile, tiles_per_seq):
    i = pl.program_id(0)

    @pl.when(i % tiles_per_seq == 0)
    def _():
        buf_ref[0:HALO_A, :] = jnp.zeros((HALO_A, WIDTH), jnp.float32)

    h = h_ref[...]
    for c in range(WIDTH // MXU_COLS):
        cols = slice(c * MXU_COLS, (c + 1) * MXU_COLS)
        cu = _dot(h, wc_ref[:, cols]) * _dot(h, wu_ref[:, cols])
        buf_ref[HALO_A:HALO_A + tile, cols] = cu
        conv = cw_ref[CONV_A - 1:CONV_A, cols] * cu
        for k in range(CONV_A - 1):
            off = HALO_A - (CONV_A - 1) + k
            conv = conv + cw_ref[k:k + 1, cols] * buf_ref[off:off + tile, cols]
        buf_ref[0:HALO_A, cols] = buf_ref[tile:tile + HALO_A, cols]
        y = _dot(h, wb_ref[:, cols]) * conv
        z_ref[:, cols] = (_silu(_dot(h, wz_ref[:, cols])) * y).astype(z_ref.dtype)


def _mixer_a_call(h, w_in, conv_w, layer, tile, tiles_per_seq):
    n = h.shape[0]

    def wspec(k):
        return pl.BlockSpec((None, D_MODEL, WIDTH),
                            lambda i: (layer, 0, OFF_A // WIDTH + k),
                            pipeline_mode=pl.Buffered(1))

    return pl.pallas_call(
        functools.partial(_mixer_a_kernel, tile=tile, tiles_per_seq=tiles_per_seq),
        grid=(n // tile,),
        in_specs=[pl.BlockSpec((tile, D_MODEL), lambda i: (i, 0)),
                  wspec(0), wspec(1), wspec(2), wspec(3),
                  pl.BlockSpec((None, CONV_A, WIDTH), lambda i: (layer, 0, 0))],
        out_specs=pl.BlockSpec((tile, WIDTH), lambda i: (i, 0)),
        out_shape=jax.ShapeDtypeStruct((n, WIDTH), jnp.bfloat16),
        scratch_shapes=[pltpu.VMEM((tile + HALO_A, WIDTH), jnp.float32)],
        compiler_params=_params(("arbitrary",)),
        name="mixer_a",
    )(h, w_in, w_in, w_in, w_in, conv_w)


def _mixer_b_kernel(h_ref, wv_ref, wg_ref, wz_ref, cw_ref, cb_ref, lg_ref, lb_ref,
                    z_ref, v_ref, o_ref, *, tile, tiles_per_seq):
    i = pl.program_id(0)
    groups = WIDTH // LANES

    @pl.when(i % tiles_per_seq == 0)
    def _():
        v_ref[0:HALO_B * groups, :] = jnp.zeros((HALO_B * groups, LANES), jnp.float32)

    h = h_ref[...]
    per = MXU_COLS // LANES
    for c in range(WIDTH // MXU_COLS):
        cols = slice(c * MXU_COLS, (c + 1) * MXU_COLS)
        v = _dot(h, wv_ref[:, cols]) * jax.nn.sigmoid(_dot(h, wg_ref[:, cols]))
        for p in range(per):
            v_ref[pl.ds(HALO_B * groups + c * per + p, tile, stride=groups), :] = (
                v[:, p * LANES:(p + 1) * LANES])

    taps = [cw_ref[k * groups:(k + 1) * groups, :] for k in range(CONV_B)]
    unroll = 8

    def body(tb, carry):
        for r in range(unroll):
            t = tb * unroll + r
            acc = None
            for k in range(CONV_B):
                row = pl.multiple_of((t + HALO_B - (CONV_B - 1) + k) * groups, groups)
                term = taps[k] * v_ref[pl.ds(row, groups), :]
                acc = term if acc is None else acc + term
            o_ref[pl.ds(pl.multiple_of(t * groups, groups), groups), :] = acc
        return carry

    lax.fori_loop(0, tile // unroll, body, 0)
    v_ref[0:HALO_B * groups, :] = v_ref[tile * groups:(tile + HALO_B) * groups, :]

    def project_z(c):
        return _dot(h, wz_ref[:, c * MXU_COLS:(c + 1) * MXU_COLS])

    n_chunks = WIDTH // MXU_COLS
    pz = {0: project_z(0)}
    cv = jnp.concatenate(
        [o_ref[pl.ds(g, tile, stride=groups), :] for g in range(groups)], axis=1)
    cv = cv + cb_ref[...]
    mu = jnp.mean(cv, axis=-1, keepdims=True)
    cen = cv - mu
    var = jnp.mean(cen * cen, axis=-1, keepdims=True)
    rstd = lax.rsqrt(var + LN_EPS)
    for c in range(n_chunks):
        if c + 1 < n_chunks:
            pz[c + 1] = project_z(c + 1)
        cols = slice(c * MXU_COLS, (c + 1) * MXU_COLS)
        ln = cen[:, cols] * rstd * lg_ref[:, cols] + lb_ref[:, cols]
        z_ref[:, cols] = (_silu(pz[c]) * _silu(ln)).astype(z_ref.dtype)


def _mixer_b_call(h, w_in, conv_w3, conv_b, ln_g, ln_b, layer, tile, tiles_per_seq):
    n = h.shape[0]
    groups = WIDTH // LANES

    def wspec(k):
        return pl.BlockSpec((None, D_MODEL, WIDTH),
                            lambda i: (layer, 0, OFF_B // WIDTH + k),
                            pipeline_mode=pl.Buffered(1))

    def vec():
        return pl.BlockSpec((None, 1, WIDTH), lambda i: (layer, 0, 0))

    return pl.pallas_call(
        functools.partial(_mixer_b_kernel, tile=tile, tiles_per_seq=tiles_per_seq),
        grid=(n // tile,),
        in_specs=[pl.BlockSpec((tile, D_MODEL), lambda i: (i, 0)),
                  wspec(0), wspec(1), wspec(2),
                  pl.BlockSpec((None, CONV_B * groups, LANES), lambda i: (layer, 0, 0)),
                  vec(), vec(), vec()],
        out_specs=pl.BlockSpec((tile, WIDTH), lambda i: (i, 0)),
        out_shape=jax.ShapeDtypeStruct((n, WIDTH), jnp.bfloat16),
        scratch_shapes=[pltpu.VMEM(((tile + HALO_B) * groups, LANES), jnp.float32),
                        pltpu.VMEM((tile * groups, LANES), jnp.float32)],
        compiler_params=_params(("arbitrary",)),
        name="mixer_b",
    )(h, w_in, w_in, w_in, conv_w3, conv_b, ln_g, ln_b)


def _mixer_c_kernel(h_ref, wu_ref, wz_ref, gw_ref, sc_ref, z_ref, buf_ref,
                    *, tile, tiles_per_seq):
    i = pl.program_id(0)
    seq_tile = i % tiles_per_seq
    n_stage = buf_ref.shape[0]

    @pl.when(seq_tile == 0)
    def _():
        buf_ref[:, 0:HALO_C, :] = jnp.zeros((n_stage, HALO_C, WIDTH), jnp.float32)

    h = h_ref[...]
    head_pos = (seq_tile * tile + 1
                + lax.broadcasted_iota(jnp.int32, (HALO_C * 2, C_GROUP), 0)
                ).astype(jnp.float32)
    def project(g):
        cols = slice(g * C_GROUP, (g + 1) * C_GROUP)
        return _dot(h, wu_ref[:, cols]), _dot(h, wz_ref[:, cols])

    def pool(g, u):
        win = POOL_WINDOWS[g]
        stages = win.bit_length() - 1
        assert win == 1 << stages and win <= 2 * HALO_C and stages <= n_stage
        cols = slice(g * C_GROUP, (g + 1) * C_GROUP)
        total = u
        for k in range(stages):
            shift = 1 << k
            buf_ref[k, HALO_C:HALO_C + tile, cols] = total
            total = total + buf_ref[k, HALO_C - shift:HALO_C - shift + tile, cols]
            buf_ref[k, 0:HALO_C, cols] = buf_ref[k, tile:tile + HALO_C, cols]
        inv_cnt = jnp.concatenate(
            [1.0 / jnp.minimum(head_pos, float(win)),
             jnp.full((tile - HALO_C * 2, C_GROUP), 1.0 / win, jnp.float32)], axis=0)
        return (total * inv_cnt - u).astype(jnp.bfloat16)

    def finish(g, d, cz):
        cols = slice(g * C_GROUP, (g + 1) * C_GROUP)
        y = _dot(d, gw_ref[g]) * sc_ref[:, cols]
        z_ref[:, cols] = (_silu(cz) * y).astype(z_ref.dtype)

    n_groups = len(POOL_WINDOWS)
    proj = {0: project(0)}
    pooled = {}
    for g in range(n_groups):
        if g + 1 < n_groups:
            proj[g + 1] = project(g + 1)
        pooled[g] = pool(g, proj[g][0])
        if g >= 1:
            finish(g - 1, pooled[g - 1], proj[g - 1][1])
    finish(n_groups - 1, pooled[n_groups - 1], proj[n_groups - 1][1])


def _mixer_c_call(h, w_in, group_w, scale, layer, tile, tiles_per_seq):
    n = h.shape[0]

    def wspec(k):
        return pl.BlockSpec((None, D_MODEL, WIDTH),
                            lambda i: (layer, 0, OFF_C // WIDTH + k),
                            pipeline_mode=pl.Buffered(1))

    return pl.pallas_call(
        functools.partial(_mixer_c_kernel, tile=tile, tiles_per_seq=tiles_per_seq),
        grid=(n // tile,),
        in_specs=[pl.BlockSpec((tile, D_MODEL), lambda i: (i, 0)),
                  wspec(0), wspec(1),
                  pl.BlockSpec((None, len(POOL_WINDOWS), C_GROUP, C_GROUP),
                               lambda i: (layer, 0, 0, 0)),
                  pl.BlockSpec((None, 1, WIDTH), lambda i: (layer, 0, 0))],
        out_specs=pl.BlockSpec((tile, WIDTH), lambda i: (i, 0)),
        out_shape=jax.ShapeDtypeStruct((n, WIDTH), jnp.bfloat16),
        scratch_shapes=[pltpu.VMEM((max(POOL_WINDOWS).bit_length() - 1, tile + HALO_C, WIDTH),
                                   jnp.float32)],
        compiler_params=_params(("arbitrary",)),
        name="mixer_c",
    )(h, w_in, w_in, group_w, scale)


def _merge_kernel(h_ref, za_ref, zb_ref, zc_ref, ga_ref, gb_ref, gc_ref,
                  wa_ref, wb_ref, wc_ref, m_ref, *, sub):
    h = h_ref[...]
    for s in range(m_ref.shape[1] // sub):
        cols = slice(s * sub, (s + 1) * sub)
        m = jax.nn.sigmoid(_dot(h, ga_ref[:, cols])) * _dot(za_ref[...], wa_ref[:, cols])
        m = m + jax.nn.sigmoid(_dot(h, gb_ref[:, cols])) * _dot(zb_ref[...], wb_ref[:, cols])
        m = m + jax.nn.sigmoid(_dot(h, gc_ref[:, cols])) * _dot(zc_ref[...], wc_ref[:, cols])
        m_ref[:, cols] = m.astype(m_ref.dtype)


def _merge_call(h, za, zb, zc, w_in, w_br_a, w_br_b, w_br_c, layer, tile, chunk, sub):
    n = h.shape[0]

    def gspec(k):
        base = (OFF_G + k * D_MODEL) // chunk
        return pl.BlockSpec((None, D_MODEL, chunk), lambda c, i: (layer, 0, base + c),
                            pipeline_mode=pl.Buffered(1))

    def bspec():
        return pl.BlockSpec((None, WIDTH, chunk), lambda c, i: (layer, 0, c),
                            pipeline_mode=pl.Buffered(1))

    def zspec():
        return pl.BlockSpec((tile, WIDTH), lambda c, i: (i, 0))

    return pl.pallas_call(
        functools.partial(_merge_kernel, sub=sub),
        grid=(D_MODEL // chunk, n // tile),
        in_specs=[pl.BlockSpec((tile, D_MODEL), lambda c, i: (i, 0)),
                  zspec(), zspec(), zspec(),
                  gspec(0), gspec(1), gspec(2),
                  bspec(), bspec(), bspec()],
        out_specs=pl.BlockSpec((tile, chunk), lambda c, i: (i, c)),
        out_shape=jax.ShapeDtypeStruct((n, D_MODEL), jnp.bfloat16),
        compiler_params=_params(("arbitrary", "arbitrary")),
        name="merge",
    )(h, za, zb, zc, w_in, w_in, w_in, w_br_a, w_br_b, w_br_c)


def _out_kernel(x_ref, m_ref, wo_ref, g_ref, *refs, emit_x, sub):
    xs_ref = refs[-1]
    hn_ref = refs[-2]
    m = m_ref[...]
    sumsq = None
    n_sub = D_MODEL // sub
    proj = {0: _dot(m, wo_ref[:, 0:sub])}
    for s in range(n_sub):
        if s + 1 < n_sub:
            proj[s + 1] = _dot(m, wo_ref[:, (s + 1) * sub:(s + 2) * sub])
        cols = slice(s * sub, (s + 1) * sub)
        x = x_ref[:, cols] + proj[s]
        xs_ref[:, cols] = x
        if emit_x:
            refs[0][:, cols] = x
        part = jnp.sum(x * x, axis=-1, keepdims=True)
        sumsq = part if sumsq is None else sumsq + part
    scale = lax.rsqrt(sumsq * (1.0 / D_MODEL) + RMS_EPS)
    hn_ref[...] = (xs_ref[...] * scale * g_ref[...]).astype(hn_ref.dtype)


def _out_call(x2, merged, w_o, g, layer, tile, emit_x, hn_dtype):
    n = x2.shape[0]
    row = pl.BlockSpec((tile, D_MODEL), lambda i: (i, 0))
    out_shape = [jax.ShapeDtypeStruct((n, D_MODEL), hn_dtype)]
    out_specs = [row]
    if emit_x:
        out_shape = [jax.ShapeDtypeStruct((n, D_MODEL), jnp.float32)] + out_shape
        out_specs = [row] + out_specs
    return pl.pallas_call(
        functools.partial(_out_kernel, emit_x=emit_x, sub=2 * MXU_COLS),
        grid=(n // tile,),
        in_specs=[row, row,
                  pl.BlockSpec((None, D_MODEL, D_MODEL), lambda i: (layer, 0, 0),
                               pipeline_mode=pl.Buffered(1)),
                  pl.BlockSpec((1, D_MODEL), lambda i: (0, 0))],
        out_specs=out_specs,
        out_shape=out_shape,
        scratch_shapes=[pltpu.VMEM((tile, D_MODEL), jnp.float32)],
        compiler_params=_params(("arbitrary",)),
        name="out_proj",
    )(x2, merged, w_o, g)


def kernel(x, norm_g, w_in, a_conv_w, b_conv_w, b_conv_b, b_ln_g, b_ln_b, c_group_w,
           c_scale, w_br_a, w_br_b, w_br_c, w_o, final_g):
    bsz, seq, _ = x.shape
    depth = w_in.shape[0]
    n = bsz * seq
    tile = 512
    assert seq % tile == 0
    groups = WIDTH // LANES
    bf16 = jnp.bfloat16

    w_in_h = w_in.astype(bf16)
    w_br_a_h, w_br_b_h, w_br_c_h = (w.astype(bf16) for w in (w_br_a, w_br_b, w_br_c))
    w_o_h = w_o.astype(bf16)
    group_w_h = c_group_w.astype(bf16)
    conv_w3 = b_conv_w.reshape(depth, CONV_B * groups, LANES)
    vec3 = lambda a: a.reshape(depth, 1, WIDTH)

    x2 = x.reshape(n, D_MODEL)
    h = _rmsnorm_call(x2, norm_g[0:1], 1024, bf16)
    for l in range(depth):
        za = _mixer_a_call(h, w_in_h, a_conv_w, l, tile, seq // tile)
        zb = _mixer_b_call(h, w_in_h, conv_w3, vec3(b_conv_b), vec3(b_ln_g),
                           vec3(b_ln_b), l, tile, seq // tile)
        zc = _mixer_c_call(h, w_in_h, group_w_h, vec3(c_scale), l, tile, seq // tile)
        merged = _merge_call(h, za, zb, zc, w_in_h, w_br_a_h, w_br_b_h, w_br_c_h,
                             l, tile, 1024, 512)
        last = l == depth - 1
        if last:
            (out,) = _out_call(x2, merged, w_o_h, final_g.reshape(1, D_MODEL), l, tile,
                               False, x.dtype)
        else:
            x2, h = _out_call(x2, merged, w_o_h, norm_g[l + 1:l + 2], l, tile, True, bf16)
    return out.reshape(bsz, seq, D_MODEL)
```
